```python
import jax
import jax.numpy as jnp
from jax import lax
import numpy as np

D_MODEL = 1024
BATCH = 8
SEQ = 4096
DEPTH = 2

CHUNK = 64
HGRN_WIDTH = D_MODEL // 2
CONV_WIDTH = D_MODEL - HGRN_WIDTH
HGRN_EXPAND = 128
HGRN_HEADS = HGRN_WIDTH // HGRN_EXPAND
HGRN_HEAD_V = HGRN_WIDTH // HGRN_HEADS
CONV_KERNEL = 31
IN_COLS = 4 * HGRN_WIDTH + 2 * CONV_WIDTH
N_EXPERTS = 16
N_EXPERT_GROUPS = 4
EXPERTS_PER_GROUP = N_EXPERTS // N_EXPERT_GROUPS
TOP_K = 2
D_EXPERT = D_MODEL // 2
ALPHA = (2 * DEPTH) ** 0.25
BETA = (8 * DEPTH) ** -0.25
EPS = 1e-5

kernel_name = "hybrid_hgrn2_conformer_conv_grouped_moe_deepnorm"


def _layernorm(x):
    xf = x.astype(jnp.float32)
    mu = jnp.mean(xf, axis=-1, keepdims=True)
    var = jnp.mean(jnp.square(xf - mu), axis=-1, keepdims=True)
    return (xf - mu) * lax.rsqrt(var + EPS)


def _modulate(x, shift, scale):
    return (_layernorm(x) * (1.0 + scale) + shift).astype(x.dtype)


def _post_ln(z, g, b):
    return (_layernorm(z) * g + b).astype(z.dtype)


def _layer_lower_bounds(lb_param):
    p = jax.nn.softmax(lb_param.astype(jnp.float32), axis=0)
    cs = jnp.cumsum(p, axis=0)
    return cs - cs[0:1]


def _hgrn2(q, f_logit, i, lb):
    B, T = q.shape[0], q.shape[1]
    n = T // CHUNK
    f32 = jnp.float32
    lbf = lb.astype(f32)
    g = jnp.logaddexp(jnp.log(lbf), jnp.log1p(-lbf) + jax.nn.log_sigmoid(f_logit.astype(f32)))
    k = -jnp.expm1(g)
    qf = jax.nn.silu(q.astype(f32))
    vf = i.astype(f32)

    def to_chunks(t):
        return t.reshape(B, n, CHUNK, t.shape[2], t.shape[3]).transpose(1, 0, 3, 2, 4)

    causal = jnp.tril(jnp.ones((CHUNK, CHUNK), dtype=bool))

    def step(S, inp):
        qc, kc, vc, gc = inp
        b = jnp.cumsum(gc, axis=2)
        diff = b[:, :, :, None, :] - b[:, :, None, :, :]
        decay = jnp.exp(jnp.where(causal[:, :, None], diff, -jnp.inf))
        scores = jnp.einsum('bhtk,bhsk,bhtsk->bhts', qc, kc, decay)
        o = (jnp.einsum('bhts,bhsv->bhtv', scores, vc)
             + jnp.einsum('bhtk,bhkv->bhtv', qc * jnp.exp(b), S))
        b_last = b[:, :, -1, :]
        S_new = (jnp.exp(b_last)[..., None] * S
                 + jnp.einsum('bhsk,bhsv->bhkv', kc * jnp.exp(b_last[:, :, None, :] - b), vc))
        return S_new, o

    S0 = jnp.zeros((B, HGRN_HEADS, HGRN_EXPAND, HGRN_HEAD_V), f32)
    _, o = lax.scan(step, S0, (to_chunks(qf), to_chunks(k), to_chunks(vf), to_chunks(g)))
    return o.transpose(1, 0, 3, 2, 4).reshape(B, T, HGRN_HEADS, HGRN_HEAD_V)


def _mixer(u, w_in, b_in, lb, norm_w, conv_w, conv_b, conv_g, conv_bb, w_out, b_out):
    B, T, _ = u.shape
    dt = u.dtype
    proj = u @ w_in + b_in
    H, C = HGRN_WIDTH, CONV_WIDTH
    q, f, i, og, ca, cgt = jnp.split(proj, [H, 2 * H, 3 * H, 4 * H, 4 * H + C], axis=-1)
    hs = (B, T, HGRN_HEADS, HGRN_EXPAND)
    o = _hgrn2(q.reshape(hs), f.reshape(hs), i.reshape(B, T, HGRN_HEADS, HGRN_HEAD_V),
               lb.reshape(HGRN_HEADS, HGRN_EXPAND))
    o = o * lax.rsqrt(jnp.mean(jnp.square(o), axis=-1, keepdims=True) + EPS) * norm_w
    h_a = (o.reshape(B, T, H) * jax.nn.silu(og.astype(jnp.float32))).astype(dt)
    a = ca * jax.nn.sigmoid(cgt)
    a = lax.conv_general_dilated(a, conv_w[:, None, :], window_strides=(1,),
                                 padding=[(CONV_KERNEL - 1, 0)],
                                 dimension_numbers=('NWC', 'WIO', 'NWC'),
                                 feature_group_count=CONV_WIDTH) + conv_b
    h_b = jax.nn.silu(_layernorm(a) * conv_g + conv_bb).astype(dt)
    return jnp.concatenate([h_a, h_b], axis=-1) @ w_out + b_out


def _moe(u, router_w, router_bias, w_gate, w_up, w_down):
    B, T, D = u.shape
    dt = u.dtype
    t = u.reshape(B * T, D)
    s = jax.nn.sigmoid((t @ router_w).astype(jnp.float32))
    sb = s + router_bias.astype(jnp.float32)
    grp_score = jnp.sum(lax.top_k(sb.reshape(-1, N_EXPERT_GROUPS, EXPERTS_PER_GROUP), TOP_K)[0], axis=-1)
    gsel = jnp.argmax(grp_score, axis=-1)
    in_group = (jnp.arange(N_EXPERTS) // EXPERTS_PER_GROUP)[None, :] == gsel[:, None]
    _, idx = lax.top_k(jnp.where(in_group, sb, -jnp.inf), TOP_K)
    w_sel = jnp.take_along_axis(s, idx, axis=-1)
    w_sel = w_sel / jnp.sum(w_sel, axis=-1, keepdims=True)
    combine = jnp.sum(jax.nn.one_hot(idx, N_EXPERTS, dtype=jnp.float32) * w_sel[..., None], axis=1).astype(dt)
    y = jnp.zeros_like(t)
    for e in range(N_EXPERTS):
        h = jax.nn.silu(t @ w_gate[e]) * (t @ w_up[e])
        y = y + combine[:, e:e + 1] * (h @ w_down[e])
    return y.reshape(B, T, D)


def setup_inputs(seed: int = 0) -> dict:
    key = jax.random.key(seed)
    ks = jax.random.split(key, 24)
    f32 = jnp.float32
    D = D_MODEL

    def nrm(k, shape, s):
        return jax.random.normal(k, shape, f32) * s

    return {
        "x": nrm(ks[0], (BATCH, SEQ, D), 1.0),
        "c": nrm(ks[1], (BATCH, D), 1.0),
        "ada_w": nrm(ks[2], (DEPTH, D, 6 * D), 0.5 * D ** -0.5),
        "ada_b": nrm(ks[3], (DEPTH, 6 * D), 0.01),
        "w_in": nrm(ks[4], (DEPTH, D, IN_COLS), D ** -0.5),
        "b_in": nrm(ks[5], (DEPTH, IN_COLS), 0.01),
        "hgrn_lb": nrm(ks[6], (DEPTH, HGRN_WIDTH), 0.5),
        "hgrn_norm_w": 1.0 + nrm(ks[7], (DEPTH, HGRN_HEAD_V), 0.01),
        "conv_w": nrm(ks[8], (DEPTH, CONV_KERNEL, CONV_WIDTH), CONV_KERNEL ** -0.5),
        "conv_b": nrm(ks[9], (DEPTH, CONV_WIDTH), 0.01),
        "conv_ln_g": 1.0 + nrm(ks[10], (DEPTH, CONV_WIDTH), 0.01),
        "conv_ln_b": nrm(ks[11], (DEPTH, CONV_WIDTH), 0.01),
        "w_out": nrm(ks[12], (DEPTH, HGRN_WIDTH + CONV_WIDTH, D), BETA * D ** -0.5),
        "b_out": nrm(ks[13], (DEPTH, D), 0.01),
        "ln1_g": 1.0 + nrm(ks[14], (DEPTH, D), 0.01),
        "ln1_b": nrm(ks[15], (DEPTH, D), 0.01),
        "router_w": nrm(ks[16], (D, N_EXPERTS), D ** -0.5),
        "router_bias": nrm(ks[17], (N_EXPERTS,), 0.01),
        "w_gate": nrm(ks[18], (DEPTH, N_EXPERTS, D, D_EXPERT), D ** -0.5),
        "w_up": nrm(ks[19], (DEPTH, N_EXPERTS, D, D_EXPERT), D ** -0.5),
        "w_down": nrm(ks[20], (DEPTH, N_EXPERTS, D_EXPERT, D), BETA * D_EXPERT ** -0.5),
        "ln2_g": 1.0 + nrm(ks[21], (DEPTH, D), 0.01),
        "ln2_b": nrm(ks[22], (DEPTH, D), 0.01),
    }


def reference(x, c, ada_w, ada_b, w_in, b_in, hgrn_lb, hgrn_norm_w, conv_w, conv_b,
              conv_ln_g, conv_ln_b, w_out, b_out, ln1_g, ln1_b, router_w, router_bias,
              w_gate, w_up, w_down, ln2_g, ln2_b):
    lb_all = _layer_lower_bounds(hgrn_lb)
    cond = jax.nn.silu(c)
    for l in range(DEPTH):
        mod = cond @ ada_w[l] + ada_b[l]
        sh1, sc1, g1, sh2, sc2, g2 = jnp.split(mod[:, None, :], 6, axis=-1)
        u = _modulate(x, sh1, sc1)
        y = _mixer(u, w_in[l], b_in[l], lb_all[l], hgrn_norm_w[l], conv_w[l], conv_b[l],
                   conv_ln_g[l], conv_ln_b[l], w_out[l], b_out[l])
        x = _post_ln(ALPHA * x + g1 * y, ln1_g[l], ln1_b[l])
        u = _modulate(x, sh2, sc2)
        y = _moe(u, router_w, router_bias, w_gate[l], w_up[l], w_down[l])
        x = _post_ln(ALPHA * x + g2 * y, ln2_g[l], ln2_b[l])
    return x
```

```python
import functools

import jax
import jax.numpy as jnp
from jax import lax
from jax.experimental import pallas as pl
from jax.experimental.pallas import tpu as pltpu

f32 = jnp.float32
bf16 = jnp.bfloat16
i32 = jnp.int32

CHUNK = 64
SUB = 16
HEADS = 4
HEAD_K = 128
HEAD_V = 128
HGRN_W = HEADS * HEAD_K
CONV_W = 512
CONV_K = 31
CONV_TAIL = 32
N_EXPERTS = 16
N_GROUPS = 4
GROUP_SZ = 4
N_PAIRS = 6
N_CLASSES = N_GROUPS * N_PAIRS
EPS = 1e-5

TT = 256
TM = 256
VMEM_LIMIT = 56 * 1024 * 1024


def _ln(x):
    mu = jnp.mean(x, axis=-1, keepdims=True)
    xc = x - mu
    var = jnp.mean(xc * xc, axis=-1, keepdims=True)
    return xc * lax.rsqrt(var + EPS)


def _split3(x):
    hi = x.astype(bf16)
    r1 = x - hi.astype(f32)
    mid = r1.astype(bf16)
    lo = (r1 - mid.astype(f32)).astype(bf16)
    return hi, mid, lo


def _dot_nt(a, b):
    return lax.dot_general(a, b, (((1,), (1,)), ((), ())), preferred_element_type=f32)


def _dot_tn(a, b):
    return lax.dot_general(a, b, (((0,), (0,)), ((), ())), preferred_element_type=f32)


def _mod_kernel(c_ref, w_ref, b_ref, o_ref):
    cond = c_ref[...]
    cond = cond * jax.nn.sigmoid(cond)
    ch, cm, cl = _split3(cond)
    w = w_ref[...]
    wh, wm, wl = _split3(w)
    acc = jnp.dot(ch, wh, preferred_element_type=f32)
    acc += jnp.dot(ch, wm, preferred_element_type=f32)
    acc += jnp.dot(cm, wh, preferred_element_type=f32)
    acc += jnp.dot(cm, wm, preferred_element_type=f32)
    acc += jnp.dot(ch, wl, preferred_element_type=f32)
    acc += jnp.dot(cl, wh, preferred_element_type=f32)
    o_ref[...] = acc + b_ref[...]


def _modulation(c, ada_w, ada_b):
    depth, d, dd = ada_w.shape
    b = c.shape[0]
    tn = 1024
    return pl.pallas_call(
        _mod_kernel,
        grid=(depth, dd // tn),
        in_specs=[
            pl.BlockSpec((b, d), lambda l, j: (0, 0)),
            pl.BlockSpec((None, d, tn), lambda l, j: (l, 0, j)),
            pl.BlockSpec((None, 1, tn), lambda l, j: (l, 0, j)),
        ],
        out_specs=pl.BlockSpec((None, b, tn), lambda l, j: (l, 0, j)),
        out_shape=jax.ShapeDtypeStruct((depth, b, dd), f32),
        compiler_params=pltpu.CompilerParams(
            dimension_semantics=("arbitrary", "arbitrary"), vmem_limit_bytes=VMEM_LIMIT),
        name="adaln_modulation",
    )(c, ada_w, ada_b.reshape(depth, 1, dd))


def _hgrn_gates(proj_scr, lb_ref, qf_scr, kk_scr, g_scr):
    for h in range(HEADS):
        hs = slice(h * HEAD_K, (h + 1) * HEAD_K)
        q = proj_scr[:, hs]
        qf_scr[:, hs] = q * jax.nn.sigmoid(q)
        z = proj_scr[:, HGRN_W + h * HEAD_K:HGRN_W + (h + 1) * HEAD_K]
        lb = lb_ref[:, hs]
        t = jnp.exp(-jnp.abs(z))
        r = 1.0 / (1.0 + t)
        tr = t * r
        pos = z >= 0.0
        sig = jnp.where(pos, r, tr)
        nsig = jnp.where(pos, tr, r)
        one_m_lb = 1.0 - lb
        kk_scr[:, hs] = one_m_lb * nsig
        g_scr[:, hs] = jnp.log(lb + one_m_lb * sig)


def _chunk_cumsum(g_scr, b_scr):
    rows = lax.broadcasted_iota(i32, (TT, TT), 0)
    cols = lax.broadcasted_iota(i32, (TT, TT), 1)
    tri = jnp.where((cols <= rows) & ((cols // CHUNK) == (rows // CHUNK)), 1.0, 0.0).astype(bf16)
    gh, gm, gl = _split3(g_scr[...])
    acc = jnp.dot(tri, gh, preferred_element_type=f32)
    acc += jnp.dot(tri, gm, preferred_element_type=f32)
    acc += jnp.dot(tri, gl, preferred_element_type=f32)
    b_scr[...] = acc


def _hgrn_chunk(c, h, qf_scr, kk_scr, b_scr, proj_scr, st_scr, o_scr):
    r0 = pl.multiple_of(c * CHUNK, CHUNK)
    rows = pl.ds(r0, CHUNK)
    hs = slice(h * HEAD_K, (h + 1) * HEAD_K)
    vs = slice(2 * HGRN_W + h * HEAD_V, 2 * HGRN_W + (h + 1) * HEAD_V)
    qf = qf_scr[rows, hs]
    kk = kk_scr[rows, hs]
    b = b_scr[rows, hs]
    v = proj_scr[rows, vs]
    v16 = v.astype(bf16)
    b_last = b[CHUNK - 1:CHUNK, :]
    st = st_scr[h]

    o = _dot_nt((qf * jnp.exp(b)).astype(bf16), st.astype(bf16))

    row_in_sub = lax.broadcasted_iota(i32, (SUB, 1), 0)
    outs = []
    for blk in range(CHUNK // SUB):
        lo = blk * SUB
        qb = qf[lo:lo + SUB]
        bb = b[lo:lo + SUB]
        kb = kk[lo:lo + SUB]
        vb = v[lo:lo + SUB]
        ob = o[lo:lo + SUB]
        for s in range(SUB):
            e = jnp.exp(jnp.minimum(bb - bb[s:s + 1], 0.0))
            p = jnp.sum(qb * e * kb[s:s + 1], axis=1, keepdims=True)
            p = jnp.where(row_in_sub >= s, p, 0.0)
            ob = ob + p * vb[s:s + 1]
        if blk > 0:
            ref = b[lo - 1:lo]
            qd = (qb * jnp.exp(bb - ref)).astype(bf16)
            kd = (kk[:lo] * jnp.exp(ref - b[:lo])).astype(bf16)
            a = _dot_nt(qd, kd)
            ob = ob + jnp.dot(a.astype(bf16), v16[:lo], preferred_element_type=f32)
        outs.append(ob)
    o_scr[rows, hs] = jnp.concatenate(outs, axis=0)

    kd = (kk * jnp.exp(b_last - b)).astype(bf16)
    st_scr[h] = st * jnp.exp(b_last) + _dot_tn(v16, kd)


def _route(u2, rwt_ref, rb_ref, cls_ref, wts_ref):
    uh, um, ul = _split3(u2)
    rh, rm, rl = _split3(rwt_ref[...])
    logits = _dot_nt(rh, uh) + _dot_nt(rh, um) + _dot_nt(rm, uh)
    logits += _dot_nt(rm, um) + _dot_nt(rh, ul) + _dot_nt(rl, uh)
    s = jax.nn.sigmoid(logits)
    sb = s + rb_ref[...]
    srow = [s[e:e + 1, :] for e in range(N_EXPERTS)]
    brow = [sb[e:e + 1, :] for e in range(N_EXPERTS)]
    gscore = []
    for g in range(N_GROUPS):
        x = brow[GROUP_SZ * g:GROUP_SZ * (g + 1)]
        best = x[0] + x[1]
        for (i, j) in ((0, 2), (0, 3), (1, 2), (1, 3), (2, 3)):
            best = jnp.maximum(best, x[i] + x[j])
        gscore.append(best)
    gsel = jnp.zeros_like(gscore[0], dtype=i32)
    gbest = gscore[0]
    for g in range(1, N_GROUPS):
        better = gscore[g] > gbest
        gsel = jnp.where(better, g, gsel)
        gbest = jnp.where(better, gscore[g], gbest)
    xb, xs = [], []
    for j in range(GROUP_SZ):
        vb_, vs_ = brow[j], srow[j]
        for g in range(1, N_GROUPS):
            vb_ = jnp.where(gsel == g, brow[GROUP_SZ * g + j], vb_)
            vs_ = jnp.where(gsel == g, srow[GROUP_SZ * g + j], vs_)
        xb.append(vb_)
        xs.append(vs_)
    sel = []
    for j in range(GROUP_SZ):
        rank = jnp.zeros_like(gsel)
        for k in range(GROUP_SZ):
            if k == j:
                continue
            ahead = (xb[k] >= xb[j]) if k < j else (xb[k] > xb[j])
            rank = rank + jnp.where(ahead, 1, 0)
        sel.append(rank < 2)
    zero = jnp.zeros_like(xs[0])
    wsum = zero
    for j in range(GROUP_SZ):
        wsum = wsum + jnp.where(sel[j], xs[j], 0.0)
    e_lo = jnp.where(sel[0], 0, jnp.where(sel[1], 1, 2))
    e_hi = jnp.where(sel[3], 3, jnp.where(sel[2], 2, 1))
    s_lo = jnp.where(sel[0], xs[0], jnp.where(sel[1], xs[1], xs[2]))
    s_hi = jnp.where(sel[3], xs[3], jnp.where(sel[2], xs[2], xs[1]))
    pair = jnp.where(e_lo == 0, e_hi - 1, jnp.where(e_lo == 1, e_hi + 1, 5))
    cls_ref[...] = gsel * N_PAIRS + pair
    wts_ref[0:1, :] = s_lo / wsum
    wts_ref[1:2, :] = s_hi / wsum


def _mix_kernel(*refs, fuse_prev, alpha):
    if fuse_prev:
        (x_ref, y_ref, pmod_ref, ln2g_ref, ln2b_ref), refs = refs[:5], refs[5:]
    else:
        (x_ref,), refs = refs[:1], refs[1:]
    (mod_ref, w_in_ref, b_in_ref, lb_ref, normw_ref, convw_ref, convb_ref, convg_ref, convbb_ref,
     w_out_ref, b_out_ref, ln1g_ref, ln1b_ref, rwt_ref, rb_ref,
     x1_ref, u2_ref, cls_ref, wts_ref,
     proj_scr, qf_scr, kk_scr, g_scr, b_scr, o_scr, st_scr, abuf_scr) = refs

    @pl.when(pl.program_id(1) == 0)
    def _():
        st_scr[...] = jnp.zeros_like(st_scr)
        abuf_scr[0:CONV_TAIL, :] = jnp.zeros((CONV_TAIL, CONV_W), f32)

    x = x_ref[...]
    if fuse_prev:
        x = _ln(alpha * x + pmod_ref[5:6, :] * y_ref[...]) * ln2g_ref[...] + ln2b_ref[...]

    sh1, sc1, g1 = mod_ref[0:1, :], mod_ref[1:2, :], mod_ref[2:3, :]
    sh2, sc2 = mod_ref[3:4, :], mod_ref[4:5, :]
    u = (_ln(x) * (1.0 + sc1) + sh1).astype(bf16)
    proj_scr[...] = jnp.dot(u, w_in_ref[...], preferred_element_type=f32) + b_in_ref[...]

    _hgrn_gates(proj_scr, lb_ref, qf_scr, kk_scr, g_scr)
    _chunk_cumsum(g_scr, b_scr)

    def chunk_body(c, carry):
        for h in range(HEADS):
            _hgrn_chunk(c, h, qf_scr, kk_scr, b_scr, proj_scr, st_scr, o_scr)
        return carry
    lax.fori_loop(0, TT // CHUNK, chunk_body, 0)

    h_parts = []
    for h in range(HEADS):
        hs = slice(h * HEAD_V, (h + 1) * HEAD_V)
        o = o_scr[:, hs]
        o = o * lax.rsqrt(jnp.mean(o * o, axis=-1, keepdims=True) + EPS) * normw_ref[...]
        og = proj_scr[:, 3 * HGRN_W + h * HEAD_V:3 * HGRN_W + (h + 1) * HEAD_V]
        h_parts.append((o * (og * jax.nn.sigmoid(og))).astype(bf16))

    ca = proj_scr[:, 4 * HGRN_W:4 * HGRN_W + CONV_W]
    cg = proj_scr[:, 4 * HGRN_W + CONV_W:4 * HGRN_W + 2 * CONV_W]
    abuf_scr[CONV_TAIL:CONV_TAIL + TT, :] = ca * jax.nn.sigmoid(cg)
    acc = jnp.zeros((TT, CONV_W), f32) + convb_ref[...]
    for j in range(CONV_K):
        off = CONV_TAIL - (CONV_K - 1) + j
        acc = acc + convw_ref[j:j + 1, :] * abuf_scr[off:off + TT, :]
    tail = abuf_scr[TT:TT + CONV_TAIL, :]
    abuf_scr[0:CONV_TAIL, :] = tail
    hb = _ln(acc) * convg_ref[...] + convbb_ref[...]
    h_parts.append((hb * jax.nn.sigmoid(hb)).astype(bf16))

    hcat = jnp.concatenate(h_parts, axis=-1)
    y = jnp.dot(hcat, w_out_ref[...], preferred_element_type=f32) + b_out_ref[...]
    x1 = _ln(alpha * x + g1 * y) * ln1g_ref[...] + ln1b_ref[...]
    x1_ref[...] = x1
    u2 = _ln(x1) * (1.0 + sc2) + sh2
    u2_ref[...] = u2
    _route(u2, rwt_ref, rb_ref, cls_ref, wts_ref)


def _mixer(l, x, y_ext, mod, p, lb_all, alpha):
    depth, bsz = mod.shape[0], mod.shape[1]
    n, d = x.shape
    t = n // bsz
    nt = t // TT
    fuse_prev = y_ext is not None
    row_spec = pl.BlockSpec((TT, d), lambda b, j: (b * nt + j, 0))

    def const(shape):
        return pl.BlockSpec(shape, lambda b, j: (0,) * len(shape))

    def mod_spec(layer):
        return pl.BlockSpec((None, None, 6, d), lambda b, j: (layer, b, 0, 0))

    in_specs, args = [row_spec], [x]
    if fuse_prev:
        in_specs += [row_spec, mod_spec(l - 1), const((1, d)), const((1, d))]
        args += [y_ext, mod, p["ln2_g"][l - 1][None], p["ln2_b"][l - 1][None]]
    in_cols = p["w_in"].shape[-1]
    in_specs += [mod_spec(l), const((d, in_cols)), const((1, in_cols)), const((1, HGRN_W)),
                 const((1, HEAD_V)), const((CONV_K, CONV_W)), const((1, CONV_W)), const((1, CONV_W)),
                 const((1, CONV_W)), const((d, d)), const((1, d)), const((1, d)), const((1, d)),
                 const((N_EXPERTS, d)), const((N_EXPERTS, 1))]
    args += [mod, p["w_in"][l], p["b_in"][l][None], lb_all[l][None], p["hgrn_norm_w"][l][None],
             p["conv_w"][l], p["conv_b"][l][None], p["conv_ln_g"][l][None], p["conv_ln_b"][l][None],
             p["w_out"][l], p["b_out"][l][None], p["ln1_g"][l][None], p["ln1_b"][l][None],
             p["router_wt"], p["router_bias"][:, None]]
    out_shape = [jax.ShapeDtypeStruct((n, d), f32), jax.ShapeDtypeStruct((n, d), f32),
                 jax.ShapeDtypeStruct((n // TT, 1, TT), i32), jax.ShapeDtypeStruct((n // TT, 2, TT), f32)]
    out_specs = [row_spec, row_spec,
                 pl.BlockSpec((None, 1, TT), lambda b, j: (b * nt + j, 0, 0)),
                 pl.BlockSpec((None, 2, TT), lambda b, j: (b * nt + j, 0, 0))]
    scratch = [pltpu.VMEM((TT, in_cols), f32), pltpu.VMEM((TT, HGRN_W), f32), pltpu.VMEM((TT, HGRN_W), f32),
               pltpu.VMEM((TT, HGRN_W), f32), pltpu.VMEM((TT, HGRN_W), f32), pltpu.VMEM((TT, HGRN_W), f32),
               pltpu.VMEM((HEADS, HEAD_V, HEAD_K), f32), pltpu.VMEM((CONV_TAIL + TT, CONV_W), f32)]
    return pl.pallas_call(
        functools.partial(_mix_kernel, fuse_prev=fuse_prev, alpha=alpha),
        grid=(bsz, nt),
        in_specs=in_specs, out_specs=out_specs, out_shape=out_shape, scratch_shapes=scratch,
        compiler_params=pltpu.CompilerParams(
            dimension_semantics=("arbitrary", "arbitrary"), vmem_limit_bytes=VMEM_LIMIT),
        name=f"mixer_l{l}",
    )(*args)


def _moe_kernel(e1_ref, e2_ref, nused_ref, src_ref, dst_ref, w_ref, u_hbm,
                wg1, wu1, wd1, wg2, wu2, wd2, y_hbm, xbuf, obuf, sem_in, sem_out):
    i = pl.program_id(0)

    @pl.when(i < nused_ref[0])
    def _():
        def gather(r, carry):
            pltpu.make_async_copy(u_hbm.at[pl.ds(src_ref[0, r], 1)], xbuf.at[pl.ds(r, 1)], sem_in).start()
            return carry
        lax.fori_loop(0, TM, gather, 0, unroll=8)

        def gather_wait(r, carry):
            pltpu.make_async_copy(u_hbm.at[pl.ds(0, 1)], xbuf.at[pl.ds(r, 1)], sem_in).wait()
            return carry
        lax.fori_loop(0, TM, gather_wait, 0, unroll=8)

        x = xbuf[...].astype(bf16)
        wt = w_ref[...].T
        h1 = jnp.dot(x, wg1[...], preferred_element_type=f32)
        h1 = (h1 * jax.nn.sigmoid(h1) * jnp.dot(x, wu1[...], preferred_element_type=f32)).astype(bf16)
        h2 = jnp.dot(x, wg2[...], preferred_element_type=f32)
        h2 = (h2 * jax.nn.sigmoid(h2) * jnp.dot(x, wu2[...], preferred_element_type=f32)).astype(bf16)
        obuf[...] = (wt[:, 0:1] * jnp.dot(h1, wd1[...], preferred_element_type=f32)
                     + wt[:, 1:2] * jnp.dot(h2, wd2[...], preferred_element_type=f32))

        def scatter(r, carry):
            pltpu.make_async_copy(obuf.at[pl.ds(r, 1)], y_hbm.at[pl.ds(dst_ref[0, r], 1)], sem_out).start()
            return carry
        lax.fori_loop(0, TM, scatter, 0, unroll=8)

        def scatter_wait(r, carry):
            pltpu.make_async_copy(obuf.at[pl.ds(r, 1)], y_hbm.at[pl.ds(0, 1)], sem_out).wait()
            return carry
        lax.fori_loop(0, TM, scatter_wait, 0, unroll=8)

    @pl.when(i >= nused_ref[0])
    def _():
        obuf[...] = jnp.zeros_like(obuf)
        fill = pltpu.make_async_copy(obuf, y_hbm.at[pl.ds(pl.multiple_of(i * TM, TM), TM)], sem_out)
        fill.start()
        fill.wait()


def _experts(u2, tile_e1, tile_e2, n_used, src, dst, w_sorted, wg, wu, wd):
    n, d = u2.shape
    n_tiles = src.shape[0]
    de = wg.shape[-1]
    n_ext = n + N_CLASSES * TM

    def wspec(shape, which):
        if which == 1:
            return pl.BlockSpec(shape, lambda i, e1, e2, nu: (e1[i], 0, 0))
        return pl.BlockSpec(shape, lambda i, e1, e2, nu: (e2[i], 0, 0))

    smem_spec = pl.BlockSpec((None, 1, TM), lambda i, e1, e2, nu: (i, 0, 0), memory_space=pltpu.SMEM)
    grid_spec = pltpu.PrefetchScalarGridSpec(
        num_scalar_prefetch=3,
        grid=(n_tiles,),
        in_specs=[smem_spec, smem_spec,
                  pl.BlockSpec((None, 8, TM), lambda i, e1, e2, nu: (i, 0, 0)),
                  pl.BlockSpec(memory_space=pl.ANY),
                  wspec((None, d, de), 1), wspec((None, d, de), 1), wspec((None, de, d), 1),
                  wspec((None, d, de), 2), wspec((None, d, de), 2), wspec((None, de, d), 2)],
        out_specs=pl.BlockSpec(memory_space=pl.ANY),
        scratch_shapes=[pltpu.VMEM((TM, d), f32), pltpu.VMEM((TM, d), f32),
                        pltpu.SemaphoreType.DMA(()), pltpu.SemaphoreType.DMA(())],
    )
    return pl.pallas_call(
        _moe_kernel,
        grid_spec=grid_spec,
        out_shape=jax.ShapeDtypeStruct((n_ext, d), f32),
        compiler_params=pltpu.CompilerParams(
            dimension_semantics=("arbitrary",), vmem_limit_bytes=VMEM_LIMIT),
        name="routed_experts",
    )(tile_e1, tile_e2, n_used, src, dst, w_sorted, u2, wg, wu, wd, wg, wu, wd)


def _dispatch_plan(cls, w_lo, w_hi):
    n = cls.shape[0]
    n_tiles = n // TM + N_CLASSES
    p = n_tiles * TM
    onehot = (cls[:, None] == jnp.arange(N_CLASSES, dtype=i32)[None, :]).astype(i32)
    csum = jnp.cumsum(onehot, axis=0)
    counts = csum[-1]
    rank = jnp.sum(csum * onehot, axis=1) - 1
    tiles_per = (counts + TM - 1) // TM
    tile_end = jnp.cumsum(tiles_per)
    tile_start = tile_end - tiles_per
    pos = tile_start[cls] * TM + rank
    n_used = tile_end[-1:]
    tile_cls = jnp.minimum(jnp.searchsorted(tile_end, jnp.arange(n_tiles, dtype=i32), side="right"),
                           N_CLASSES - 1).astype(i32)
    tok = jnp.arange(n, dtype=i32)
    src = jnp.zeros((p,), i32).at[pos].set(tok)
    valid = jnp.zeros((p,), i32).at[pos].set(1)
    pad_rank = jnp.cumsum(1 - valid) - 1
    slot = jnp.arange(p, dtype=i32)
    dst = jnp.where(valid == 1, src, jnp.where(slot < n_used[0] * TM, n + pad_rank, slot))
    ws = jnp.zeros((2, p), f32).at[:, pos].set(jnp.stack([w_lo, w_hi]))
    w_sorted = jnp.zeros((n_tiles, 8, TM), f32).at[:, 0:2, :].set(
        ws.reshape(2, n_tiles, TM).transpose(1, 0, 2))
    grp = tile_cls // N_PAIRS
    pair = tile_cls % N_PAIRS
    lo_tab = jnp.array([0, 0, 0, 1, 1, 2], i32)
    hi_tab = jnp.array([1, 2, 3, 2, 3, 3], i32)
    tile_e1 = grp * GROUP_SZ + lo_tab[pair]
    tile_e2 = grp * GROUP_SZ + hi_tab[pair]
    return (tile_e1, tile_e2, n_used.astype(i32), src.reshape(n_tiles, 1, TM), dst.reshape(n_tiles, 1, TM),
            w_sorted)


def _post_kernel(x_ref, y_ref, mod_ref, g_ref, b_ref, o_ref, *, alpha):
    z = alpha * x_ref[...] + mod_ref[5:6, :] * y_ref[...]
    o_ref[...] = _ln(z) * g_ref[...] + b_ref[...]


def _post_ln(l, x1, y_ext, mod, g, b, alpha):
    bsz = mod.shape[1]
    n, d = x1.shape
    tr = 512
    nt = n // bsz // tr
    row_spec = pl.BlockSpec((tr, d), lambda bb, j: (bb * nt + j, 0))
    return pl.pallas_call(
        functools.partial(_post_kernel, alpha=alpha),
        grid=(bsz, nt),
        in_specs=[row_spec, row_spec,
                  pl.BlockSpec((None, None, 6, d), lambda bb, j: (l, bb, 0, 0)),
                  pl.BlockSpec((1, d), lambda bb, j: (0, 0)), pl.BlockSpec((1, d), lambda bb, j: (0, 0))],
        out_specs=row_spec,
        out_shape=jax.ShapeDtypeStruct((n, d), f32),
        compiler_params=pltpu.CompilerParams(
            dimension_semantics=("arbitrary", "arbitrary"), vmem_limit_bytes=VMEM_LIMIT),
        name="final_post_ln",
    )(x1, y_ext, mod, g[None], b[None])


def kernel(x, c, ada_w, ada_b, w_in, b_in, hgrn_lb, hgrn_norm_w, conv_w, conv_b, conv_ln_g, conv_ln_b,
           w_out, b_out, ln1_g, ln1_b, router_w, router_bias, w_gate, w_up, w_down, ln2_g, ln2_b):
    bsz, t, d = x.shape
    depth = ada_w.shape[0]
    n = bsz * t
    assert t % TT == 0 and n % TM == 0 and d == 2 * HGRN_W
    alpha = (2 * depth) ** 0.25

    pr = jax.nn.softmax(hgrn_lb.astype(f32), axis=0)
    cs = jnp.cumsum(pr, axis=0)
    lb_all = cs - cs[0:1]

    mod = _modulation(c, ada_w, ada_b).reshape(depth, bsz, 6, d)
    p = dict(w_in=w_in.astype(bf16), b_in=b_in, hgrn_norm_w=hgrn_norm_w, conv_w=conv_w, conv_b=conv_b,
             conv_ln_g=conv_ln_g, conv_ln_b=conv_ln_b, w_out=w_out.astype(bf16), b_out=b_out,
             ln1_g=ln1_g, ln1_b=ln1_b, ln2_g=ln2_g, ln2_b=ln2_b,
             router_wt=router_w.T, router_bias=router_bias)
    wg16, wu16, wd16 = w_gate.astype(bf16), w_up.astype(bf16), w_down.astype(bf16)

    xcur = x.reshape(n, d)
    y_ext = None
    for l in range(depth):
        x1, u2, cls, wts = _mixer(l, xcur, y_ext, mod, p, lb_all, alpha)
        plan = _dispatch_plan(cls.reshape(n), wts[:, 0, :].reshape(n), wts[:, 1, :].reshape(n))
        y_ext = _experts(u2, *plan, wg16[l], wu16[l], wd16[l])
        xcur = x1
    out = _post_ln(depth - 1, xcur, y_ext, mod, ln2_g[depth - 1], ln2_b[depth - 1], alpha)
    return out.reshape(bsz, t, d)
```

```python
import functools

import jax
import jax.numpy as jnp
from jax import lax
from jax.experimental import pallas as pl
from jax.experimental.pallas import tpu as pltpu

f32 = jnp.float32
bf16 = jnp.bfloat16
i32 = jnp.int32

CHUNK = 64
SUB = 16
HEADS = 4
HEAD_K = 128
HEAD_V = 128
HGRN_W = HEADS * HEAD_K
CONV_W = 512
CONV_K = 31
CONV_TAIL = 32
N_EXPERTS = 16
N_GROUPS = 4
GROUP_SZ = 4
N_PAIRS = 6
N_CLASSES = N_GROUPS * N_PAIRS
CLS_PAD = 32
EPS = 1e-5

TT = 256
TM = 256
VMEM_LIMIT = 56 * 1024 * 1024


def _ln(x):
    mu = jnp.mean(x, axis=-1, keepdims=True)
    xc = x - mu
    var = jnp.mean(xc * xc, axis=-1, keepdims=True)
    return xc * lax.rsqrt(var + EPS)


def _split3(x):
    hi = x.astype(bf16)
    r1 = x - hi.astype(f32)
    mid = r1.astype(bf16)
    lo = (r1 - mid.astype(f32)).astype(bf16)
    return hi, mid, lo


def _dot_nt(a, b):
    return lax.dot_general(a, b, (((1,), (1,)), ((), ())), preferred_element_type=f32)


def _dot_tn(a, b):
    return lax.dot_general(a, b, (((0,), (0,)), ((), ())), preferred_element_type=f32)


def _mod_kernel(c_ref, w_ref, b_ref, o_ref):
    cond = c_ref[...]
    cond = cond * jax.nn.sigmoid(cond)
    ch, cm, cl = _split3(cond)
    w = w_ref[...]
    wh, wm, wl = _split3(w)
    acc = jnp.dot(ch, wh, preferred_element_type=f32)
    acc += jnp.dot(ch, wm, preferred_element_type=f32)
    acc += jnp.dot(cm, wh, preferred_element_type=f32)
    acc += jnp.dot(cm, wm, preferred_element_type=f32)
    acc += jnp.dot(ch, wl, preferred_element_type=f32)
    acc += jnp.dot(cl, wh, preferred_element_type=f32)
    o_ref[...] = acc + b_ref[...]


def _modulation(c, ada_w, ada_b):
    depth, d, dd = ada_w.shape
    b = c.shape[0]
    tn = 1024
    return pl.pallas_call(
        _mod_kernel,
        grid=(depth, dd // tn),
        in_specs=[
            pl.BlockSpec((b, d), lambda l, j: (0, 0)),
            pl.BlockSpec((None, d, tn), lambda l, j: (l, 0, j)),
            pl.BlockSpec((None, 1, tn), lambda l, j: (l, 0, j)),
        ],
        out_specs=pl.BlockSpec((None, b, tn), lambda l, j: (l, 0, j)),
        out_shape=jax.ShapeDtypeStruct((depth, b, dd), f32),
        compiler_params=pltpu.CompilerParams(
            dimension_semantics=("arbitrary", "arbitrary"), vmem_limit_bytes=VMEM_LIMIT),
        name="adaln_modulation",
    )(c, ada_w, ada_b.reshape(depth, 1, dd))


def _hgrn_gates(proj_scr, lb_ref, qf_scr, kk_scr, g_scr):
    for h in range(HEADS):
        hs = slice(h * HEAD_K, (h + 1) * HEAD_K)
        q = proj_scr[:, hs]
        qf_scr[:, hs] = q * jax.nn.sigmoid(q)
        z = proj_scr[:, HGRN_W + h * HEAD_K:HGRN_W + (h + 1) * HEAD_K]
        lb = lb_ref[:, hs]
        t = jnp.exp(-jnp.abs(z))
        r = 1.0 / (1.0 + t)
        tr = t * r
        pos = z >= 0.0
        sig = jnp.where(pos, r, tr)
        nsig = jnp.where(pos, tr, r)
        one_m_lb = 1.0 - lb
        kk_scr[:, hs] = one_m_lb * nsig
        g_scr[:, hs] = jnp.log(lb + one_m_lb * sig)


def _chunk_cumsum(g_scr, b_scr):
    rows = lax.broadcasted_iota(i32, (TT, TT), 0)
    cols = lax.broadcasted_iota(i32, (TT, TT), 1)
    tri = jnp.where((cols <= rows) & ((cols // CHUNK) == (rows // CHUNK)), 1.0, 0.0).astype(bf16)
    gh, gm, gl = _split3(g_scr[...])
    acc = jnp.dot(tri, gh, preferred_element_type=f32)
    acc += jnp.dot(tri, gm, preferred_element_type=f32)
    acc += jnp.dot(tri, gl, preferred_element_type=f32)
    b_scr[...] = acc


def _hgrn_chunk(c, h, qf_scr, kk_scr, b_scr, proj_scr, st_scr, o_scr):
    r0 = pl.multiple_of(c * CHUNK, CHUNK)
    rows = pl.ds(r0, CHUNK)
    hs = slice(h * HEAD_K, (h + 1) * HEAD_K)
    vs = slice(2 * HGRN_W + h * HEAD_V, 2 * HGRN_W + (h + 1) * HEAD_V)
    qf = qf_scr[rows, hs]
    kk = kk_scr[rows, hs]
    b = b_scr[rows, hs]
    v = proj_scr[rows, vs]
    v16 = v.astype(bf16)
    b_last = b[CHUNK - 1:CHUNK, :]
    st = st_scr[h]

    o = _dot_nt((qf * jnp.exp(b)).astype(bf16), st.astype(bf16))

    row_in_sub = lax.broadcasted_iota(i32, (SUB, 1), 0)
    outs = []
    for blk in range(CHUNK // SUB):
        lo = blk * SUB
        qb = qf[lo:lo + SUB]
        bb = b[lo:lo + SUB]
        kb = kk[lo:lo + SUB]
        vb = v[lo:lo + SUB]
        ob = o[lo:lo + SUB]
        for s in range(SUB):
            e = jnp.exp(jnp.minimum(bb - bb[s:s + 1], 0.0))
            p = jnp.sum(qb * e * kb[s:s + 1], axis=1, keepdims=True)
            p = jnp.where(row_in_sub >= s, p, 0.0)
            ob = ob + p * vb[s:s + 1]
        if blk > 0:
            ref = b[lo - 1:lo]
            qd = (qb * jnp.exp(bb - ref)).astype(bf16)
            kd = (kk[:lo] * jnp.exp(ref - b[:lo])).astype(bf16)
            a = _dot_nt(qd, kd)
            ob = ob + jnp.dot(a.astype(bf16), v16[:lo], preferred_element_type=f32)
        outs.append(ob)
    o_scr[rows, hs] = jnp.concatenate(outs, axis=0)

    kd = (kk * jnp.exp(b_last - b)).astype(bf16)
    st_scr[h] = st * jnp.exp(b_last) + _dot_tn(v16, kd)


def _route(u2, rwt_ref, rb_ref, cls_ref, rank_ref, cnt_ref):
    uh, um, ul = _split3(u2)
    rh, rm, rl = _split3(rwt_ref[...])
    logits = _dot_nt(rh, uh) + _dot_nt(rh, um) + _dot_nt(rm, uh)
    logits += _dot_nt(rm, um) + _dot_nt(rh, ul) + _dot_nt(rl, uh)
    s = jax.nn.sigmoid(logits)
    sb = s + rb_ref[...]
    srow = [s[e:e + 1, :] for e in range(N_EXPERTS)]
    brow = [sb[e:e + 1, :] for e in range(N_EXPERTS)]
    gscore = []
    for g in range(N_GROUPS):
        x = brow[GROUP_SZ * g:GROUP_SZ * (g + 1)]
        best = x[0] + x[1]
        for (i, j) in ((0, 2), (0, 3), (1, 2), (1, 3), (2, 3)):
            best = jnp.maximum(best, x[i] + x[j])
        gscore.append(best)
    gsel = jnp.zeros_like(gscore[0], dtype=i32)
    gbest = gscore[0]
    for g in range(1, N_GROUPS):
        better = gscore[g] > gbest
        gsel = jnp.where(better, g, gsel)
        gbest = jnp.where(better, gscore[g], gbest)
    xb, xs = [], []
    for j in range(GROUP_SZ):
        vb_, vs_ = brow[j], srow[j]
        for g in range(1, N_GROUPS):
            vb_ = jnp.where(gsel == g, brow[GROUP_SZ * g + j], vb_)
            vs_ = jnp.where(gsel == g, srow[GROUP_SZ * g + j], vs_)
        xb.append(vb_)
        xs.append(vs_)
    sel = []
    for j in range(GROUP_SZ):
        rank = jnp.zeros_like(gsel)
        for k in range(GROUP_SZ):
            if k == j:
                continue
            ahead = (xb[k] >= xb[j]) if k < j else (xb[k] > xb[j])
            rank = rank + jnp.where(ahead, 1, 0)
        sel.append(rank < 2)
    zero = jnp.zeros_like(xs[0])
    wsum = zero
    for j in range(GROUP_SZ):
        wsum = wsum + jnp.where(sel[j], xs[j], 0.0)
    e_lo = jnp.where(sel[0], 0, jnp.where(sel[1], 1, 2))
    e_hi = jnp.where(sel[3], 3, jnp.where(sel[2], 2, 1))
    s_lo = jnp.where(sel[0], xs[0], jnp.where(sel[1], xs[1], xs[2]))
    s_hi = jnp.where(sel[3], xs[3], jnp.where(sel[2], xs[2], xs[1]))
    pair = jnp.where(e_lo == 0, e_hi - 1, jnp.where(e_lo == 1, e_hi + 1, 5))
    cls = gsel * N_PAIRS + pair
    cls_ref[...] = cls

    onehot = jnp.where(lax.broadcasted_iota(i32, (CLS_PAD, TT), 0) == cls, 1.0, 0.0)
    upper = jnp.where(lax.broadcasted_iota(i32, (TT, TT), 0) <= lax.broadcasted_iota(i32, (TT, TT), 1),
                      1.0, 0.0).astype(bf16)
    running = jnp.dot(onehot.astype(bf16), upper, preferred_element_type=f32)
    rank_ref[...] = (jnp.sum(onehot * running, axis=0, keepdims=True) - 1.0).astype(i32)
    cnt_ref[...] = running[:, TT - 128:]

    wrows = jnp.concatenate([s_lo / wsum, s_hi / wsum, jnp.zeros((126, TT), f32)], axis=0)
    return wrows.T


def _mix_kernel(*refs, fuse_prev, alpha):
    if fuse_prev:
        (x_ref, y_ref, pmod_ref, ln2g_ref, ln2b_ref), refs = refs[:5], refs[5:]
    else:
        (x_ref,), refs = refs[:1], refs[1:]
    (mod_ref, w_in_ref, b_in_ref, lb_ref, normw_ref, convw_ref, convb_ref, convg_ref, convbb_ref,
     w_out_ref, b_out_ref, ln1g_ref, ln1b_ref, rwt_ref, rb_ref,
     x1_ref, u2_ref, cls_ref, rank_ref, cnt_ref,
     proj_scr, qf_scr, kk_scr, g_scr, b_scr, o_scr, st_scr, abuf_scr) = refs

    @pl.when(pl.program_id(1) == 0)
    def _():
        st_scr[...] = jnp.zeros_like(st_scr)
        abuf_scr[0:CONV_TAIL, :] = jnp.zeros((CONV_TAIL, CONV_W), f32)

    x = x_ref[...]
    if fuse_prev:
        x = _ln(alpha * x + pmod_ref[5:6, :] * y_ref[...]) * ln2g_ref[...] + ln2b_ref[...]

    sh1, sc1, g1 = mod_ref[0:1, :], mod_ref[1:2, :], mod_ref[2:3, :]
    sh2, sc2 = mod_ref[3:4, :], mod_ref[4:5, :]
    u = (_ln(x) * (1.0 + sc1) + sh1).astype(bf16)
    proj_scr[...] = jnp.dot(u, w_in_ref[...], preferred_element_type=f32) + b_in_ref[...]

    _hgrn_gates(proj_scr, lb_ref, qf_scr, kk_scr, g_scr)
    _chunk_cumsum(g_scr, b_scr)

    def chunk_body(c, carry):
        for h in range(HEADS):
            _hgrn_chunk(c, h, qf_scr, kk_scr, b_scr, proj_scr, st_scr, o_scr)
        return carry
    lax.fori_loop(0, TT // CHUNK, chunk_body, 0)

    h_parts = []
    for h in range(HEADS):
        hs = slice(h * HEAD_V, (h + 1) * HEAD_V)
        o = o_scr[:, hs]
        o = o * lax.rsqrt(jnp.mean(o * o, axis=-1, keepdims=True) + EPS) * normw_ref[...]
        og = proj_scr[:, 3 * HGRN_W + h * HEAD_V:3 * HGRN_W + (h + 1) * HEAD_V]
        h_parts.append((o * (og * jax.nn.sigmoid(og))).astype(bf16))

    ca = proj_scr[:, 4 * HGRN_W:4 * HGRN_W + CONV_W]
    cg = proj_scr[:, 4 * HGRN_W + CONV_W:4 * HGRN_W + 2 * CONV_W]
    abuf_scr[CONV_TAIL:CONV_TAIL + TT, :] = ca * jax.nn.sigmoid(cg)
    acc = jnp.zeros((TT, CONV_W), f32) + convb_ref[...]
    for j in range(CONV_K):
        off = CONV_TAIL - (CONV_K - 1) + j
        acc = acc + convw_ref[j:j + 1, :] * abuf_scr[off:off + TT, :]
    tail = abuf_scr[TT:TT + CONV_TAIL, :]
    abuf_scr[0:CONV_TAIL, :] = tail
    hb = _ln(acc) * convg_ref[...] + convbb_ref[...]
    h_parts.append((hb * jax.nn.sigmoid(hb)).astype(bf16))

    hcat = jnp.concatenate(h_parts, axis=-1)
    y = jnp.dot(hcat, w_out_ref[...], preferred_element_type=f32) + b_out_ref[...]
    x1 = _ln(alpha * x + g1 * y) * ln1g_ref[...] + ln1b_ref[...]
    x1_ref[...] = x1
    u2 = _ln(x1) * (1.0 + sc2) + sh2
    d = u2.shape[-1]
    u2_ref[:, 0:d] = u2
    u2_ref[:, d:d + 128] = _route(u2, rwt_ref, rb_ref, cls_ref, rank_ref, cnt_ref)


def _mixer(l, x, y_ext, mod, p, lb_all, alpha):
    depth, bsz = mod.shape[0], mod.shape[1]
    n, d = x.shape
    t = n // bsz
    nt = t // TT
    fuse_prev = y_ext is not None
    row_spec = pl.BlockSpec((TT, d), lambda b, j: (b * nt + j, 0))

    def const(shape):
        return pl.BlockSpec(shape, lambda b, j: (0,) * len(shape))

    def mod_spec(layer):
        return pl.BlockSpec((None, None, 6, d), lambda b, j: (layer, b, 0, 0))

    in_specs, args = [row_spec], [x]
    if fuse_prev:
        in_specs += [row_spec, mod_spec(l - 1), const((1, d)), const((1, d))]
        args += [y_ext, mod, p["ln2_g"][l - 1][None], p["ln2_b"][l - 1][None]]
    in_cols = p["w_in"].shape[-1]
    in_specs += [mod_spec(l), const((d, in_cols)), const((1, in_cols)), const((1, HGRN_W)),
                 const((1, HEAD_V)), const((CONV_K, CONV_W)), const((1, CONV_W)), const((1, CONV_W)),
                 const((1, CONV_W)), const((d, d)), const((1, d)), const((1, d)), const((1, d)),
                 const((N_EXPERTS, d)), const((N_EXPERTS, 1))]
    args += [mod, p["w_in"][l], p["b_in"][l][None], lb_all[l][None], p["hgrn_norm_w"][l][None],
             p["conv_w"][l], p["conv_b"][l][None], p["conv_ln_g"][l][None], p["conv_ln_b"][l][None],
             p["w_out"][l], p["b_out"][l][None], p["ln1_g"][l][None], p["ln1_b"][l][None],
             p["router_wt"], p["router_bias"][:, None]]
    tile_row = pl.BlockSpec((None, 1, TT), lambda b, j: (b * nt + j, 0, 0))
    out_shape = [jax.ShapeDtypeStruct((n, d), f32), jax.ShapeDtypeStruct((n, d + 128), f32),
                 jax.ShapeDtypeStruct((n // TT, 1, TT), i32), jax.ShapeDtypeStruct((n // TT, 1, TT), i32),
                 jax.ShapeDtypeStruct((n // TT, CLS_PAD, 128), f32)]
    out_specs = [row_spec, pl.BlockSpec((TT, d + 128), lambda b, j: (b * nt + j, 0)),
                 tile_row, tile_row,
                 pl.BlockSpec((None, CLS_PAD, 128), lambda b, j: (b * nt + j, 0, 0))]
    scratch = [pltpu.VMEM((TT, in_cols), f32), pltpu.VMEM((TT, HGRN_W), f32), pltpu.VMEM((TT, HGRN_W), f32),
               pltpu.VMEM((TT, HGRN_W), f32), pltpu.VMEM((TT, HGRN_W), f32), pltpu.VMEM((TT, HGRN_W), f32),
               pltpu.VMEM((HEADS, HEAD_V, HEAD_K), f32), pltpu.VMEM((CONV_TAIL + TT, CONV_W), f32)]
    return pl.pallas_call(
        functools.partial(_mix_kernel, fuse_prev=fuse_prev, alpha=alpha),
        grid=(bsz, nt),
        in_specs=in_specs, out_specs=out_specs, out_shape=out_shape, scratch_shapes=scratch,
        compiler_params=pltpu.CompilerParams(
            dimension_semantics=("arbitrary", "arbitrary"), vmem_limit_bytes=VMEM_LIMIT),
        name=f"mixer_l{l}",
    )(*args)


def _moe_kernel(e1_ref, e2_ref, nused_ref, src_ref, nxt_ref, dst_ref, u_hbm,
                wg1, wu1, wd1, wg2, wu2, wd2, y_hbm, xbuf, obuf, sem_in, sem_out):
    i = pl.program_id(0)
    n_used = nused_ref[0]
    slot = i % 2
    d = obuf.shape[-1]

    def start_gather(idx_ref, s):
        for r in range(TM):
            pltpu.make_async_copy(u_hbm.at[pl.ds(idx_ref[0, r], 1)], xbuf.at[s, pl.ds(r, 1)],
                                  sem_in.at[s]).start(priority=r % 2)

    def wait_gather(s):
        pltpu.make_async_copy(u_hbm.at[pl.ds(0, TM)], xbuf.at[s], sem_in.at[s]).wait()

    def wait_scatter(s):
        pltpu.make_async_copy(obuf.at[s], y_hbm.at[pl.ds(0, TM)], sem_out.at[s]).wait()

    @pl.when(i == 0)
    def _():
        start_gather(src_ref, 0)

    @pl.when(i + 1 < n_used)
    def _():
        start_gather(nxt_ref, 1 - slot)

    @pl.when(i < n_used)
    def _():
        wait_gather(slot)

        @pl.when(i >= 2)
        def _():
            wait_scatter(slot)

        xw = xbuf[slot]
        x = xw[:, 0:d].astype(bf16)
        w_lo, w_hi = xw[:, d:d + 1], xw[:, d + 1:d + 2]
        h1 = jnp.dot(x, wg1[...], preferred_element_type=f32)
        h1 = (h1 * jax.nn.sigmoid(h1) * jnp.dot(x, wu1[...], preferred_element_type=f32)).astype(bf16)
        h2 = jnp.dot(x, wg2[...], preferred_element_type=f32)
        h2 = (h2 * jax.nn.sigmoid(h2) * jnp.dot(x, wu2[...], preferred_element_type=f32)).astype(bf16)
        obuf[slot] = (w_lo * jnp.dot(h1, wd1[...], preferred_element_type=f32)
                      + w_hi * jnp.dot(h2, wd2[...], preferred_element_type=f32))

        for r in range(TM):
            pltpu.make_async_copy(obuf.at[slot, pl.ds(r, 1)], y_hbm.at[pl.ds(dst_ref[0, r], 1)],
                                  sem_out.at[slot]).start(priority=r % 2)

        @pl.when(i == n_used - 1)
        def _():
            wait_scatter(slot)

            @pl.when(i >= 1)
            def _():
                wait_scatter(1 - slot)

    @pl.when(i >= n_used)
    def _():
        obuf[0] = jnp.zeros(obuf.shape[1:], f32)
        fill = pltpu.make_async_copy(obuf.at[0], y_hbm.at[pl.ds(pl.multiple_of(i * TM, TM), TM)], sem_out.at[0])
        fill.start()
        fill.wait()


def _experts(u2, tile_e1, tile_e2, n_used, src, dst, wg, wu, wd):
    n, dw = u2.shape
    d = dw - 128
    n_tiles = src.shape[0]
    de = wg.shape[-1]
    n_ext = n + N_CLASSES * TM

    def wspec(shape, which):
        if which == 1:
            return pl.BlockSpec(shape, lambda i, e1, e2, nu: (e1[i], 0, 0))
        return pl.BlockSpec(shape, lambda i, e1, e2, nu: (e2[i], 0, 0))

    cur_spec = pl.BlockSpec((None, 1, TM), lambda i, e1, e2, nu: (i, 0, 0), memory_space=pltpu.SMEM)
    nxt_spec = pl.BlockSpec((None, 1, TM), lambda i, e1, e2, nu: (jnp.minimum(i + 1, n_tiles - 1), 0, 0),
                            memory_space=pltpu.SMEM)
    grid_spec = pltpu.PrefetchScalarGridSpec(
        num_scalar_prefetch=3,
        grid=(n_tiles,),
        in_specs=[cur_spec, nxt_spec, cur_spec,
                  pl.BlockSpec(memory_space=pl.ANY),
                  wspec((None, d, de), 1), wspec((None, d, de), 1), wspec((None, de, d), 1),
                  wspec((None, d, de), 2), wspec((None, d, de), 2), wspec((None, de, d), 2)],
        out_specs=pl.BlockSpec(memory_space=pl.ANY),
        scratch_shapes=[pltpu.VMEM((2, TM, dw), f32), pltpu.VMEM((2, TM, d), f32),
                        pltpu.SemaphoreType.DMA((2,)), pltpu.SemaphoreType.DMA((2,))],
    )
    return pl.pallas_call(
        _moe_kernel,
        grid_spec=grid_spec,
        out_shape=jax.ShapeDtypeStruct((n_ext, d), f32),
        compiler_params=pltpu.CompilerParams(
            dimension_semantics=("arbitrary",), vmem_limit_bytes=VMEM_LIMIT),
        name="routed_experts",
    )(tile_e1, tile_e2, n_used, src, src, dst, u2, wg, wu, wd, wg, wu, wd)


def _dispatch_plan(cls, lrank, cnt):
    nmt = cls.shape[0]
    n = nmt * TT
    n_tiles = n // TM + N_CLASSES
    total = jnp.sum(cnt, axis=0)
    tiles_per = (total + TM - 1) // TM
    tile_end = jnp.cumsum(tiles_per)
    tile_start = tile_end - tiles_per
    n_used = tile_end[-1]
    base = tile_start[None, :] * TM + jnp.cumsum(cnt, axis=0) - cnt
    hit = cls[:, :, None] == jnp.arange(N_CLASSES, dtype=i32)[None, None, :]
    pos = lrank + jnp.sum(jnp.where(hit, base[:, None, :], 0), axis=-1)
    src = jnp.zeros((n_tiles * TM,), i32).at[pos.reshape(n)].set(jnp.arange(n, dtype=i32))
    src = src.reshape(n_tiles, TM)

    ti = jnp.arange(n_tiles, dtype=i32)
    tile_cls = jnp.minimum(jnp.sum((ti[:, None] >= tile_end[None, :]).astype(i32), axis=1), N_CLASSES - 1)
    oh = (tile_cls[:, None] == jnp.arange(N_CLASSES, dtype=i32)[None, :]).astype(i32)
    tot_t = jnp.sum(oh * total[None, :], axis=1)
    start_t = jnp.sum(oh * tile_start[None, :], axis=1)
    n_valid = jnp.clip(tot_t - (ti - start_t) * TM, 0, TM)
    pads = jnp.where(ti < n_used, TM - n_valid, 0)
    pad_before = jnp.cumsum(pads) - pads
    r = jnp.arange(TM, dtype=i32)[None, :]
    dst = jnp.where(r < n_valid[:, None], src,
                    jnp.where(ti[:, None] < n_used, n + pad_before[:, None] + r - n_valid[:, None],
                              ti[:, None] * TM + r))
    grp = tile_cls // N_PAIRS
    pair = tile_cls % N_PAIRS
    lo = jnp.where(pair < 3, 0, jnp.where(pair < 5, 1, 2))
    hi = jnp.where(pair < 3, pair + 1, jnp.where(pair < 5, pair - 1, 3))
    return (grp * GROUP_SZ + lo, grp * GROUP_SZ + hi, n_used.reshape(1).astype(i32),
            src.reshape(n_tiles, 1, TM), dst.reshape(n_tiles, 1, TM))


def _post_kernel(x_ref, y_ref, mod_ref, g_ref, b_ref, o_ref, *, alpha):
    z = alpha * x_ref[...] + mod_ref[5:6, :] * y_ref[...]
    o_ref[...] = _ln(z) * g_ref[...] + b_ref[...]


def _post_ln(l, x1, y_ext, mod, g, b, alpha):
    bsz = mod.shape[1]
    n, d = x1.shape
    tr = 512
    nt = n // bsz // tr
    row_spec = pl.BlockSpec((tr, d), lambda bb, j: (bb * nt + j, 0))
    return pl.pallas_call(
        functools.partial(_post_kernel, alpha=alpha),
        grid=(bsz, nt),
        in_specs=[row_spec, row_spec,
                  pl.BlockSpec((None, None, 6, d), lambda bb, j: (l, bb, 0, 0)),
                  pl.BlockSpec((1, d), lambda bb, j: (0, 0)), pl.BlockSpec((1, d), lambda bb, j: (0, 0))],
        out_specs=row_spec,
        out_shape=jax.ShapeDtypeStruct((n, d), f32),
        compiler_params=pltpu.CompilerParams(
            dimension_semantics=("arbitrary", "arbitrary"), vmem_limit_bytes=VMEM_LIMIT),
        name="final_post_ln",
    )(x1, y_ext, mod, g[None], b[None])


def kernel(x, c, ada_w, ada_b, w_in, b_in, hgrn_lb, hgrn_norm_w, conv_w, conv_b, conv_ln_g, conv_ln_b,
           w_out, b_out, ln1_g, ln1_b, router_w, router_bias, w_gate, w_up, w_down, ln2_g, ln2_b):
    bsz, t, d = x.shape
    depth = ada_w.shape[0]
    n = bsz * t
    assert t % TT == 0 and n % TM == 0 and d == 2 * HGRN_W
    alpha = (2 * depth) ** 0.25

    pr = jax.nn.softmax(hgrn_lb.astype(f32), axis=0)
    cs = jnp.cumsum(pr, axis=0)
    lb_all = cs - cs[0:1]

    mod = _modulation(c, ada_w, ada_b).reshape(depth, bsz, 6, d)
    p = dict(w_in=w_in.astype(bf16), b_in=b_in, hgrn_norm_w=hgrn_norm_w, conv_w=conv_w, conv_b=conv_b,
             conv_ln_g=conv_ln_g, conv_ln_b=conv_ln_b, w_out=w_out.astype(bf16), b_out=b_out,
             ln1_g=ln1_g, ln1_b=ln1_b, ln2_g=ln2_g, ln2_b=ln2_b,
             router_wt=router_w.T, router_bias=router_bias)
    wg16, wu16, wd16 = w_gate.astype(bf16), w_up.astype(bf16), w_down.astype(bf16)

    xcur = x.reshape(n, d)
    y_ext = None
    for l in range(depth):
        x1, u2, cls, lrank, cnt = _mixer(l, xcur, y_ext, mod, p, lb_all, alpha)
        plan = _dispatch_plan(cls[:, 0, :], lrank[:, 0, :], cnt[:, :N_CLASSES, 127].astype(i32))
        y_ext = _experts(u2, *plan, wg16[l], wu16[l], wd16[l])
        xcur = x1
    out = _post_ln(depth - 1, xcur, y_ext, mod, ln2_g[depth - 1], ln2_b[depth - 1], alpha)
    return out.reshape(bsz, t, d)
```
